```python
import math
import jax, jax.numpy as jnp
from jax import lax
import numpy as np

D_MODEL = 1024
BATCH = 8
SEQ = 4096
DEPTH = 2
DEC_BATCH = 16
DEC_SEQ = 4096
PAST_LEN = 128

D_FF = 2816
N_EVEN = (DEPTH + 1) // 2
N_ODD = DEPTH // 2
DA = D_MODEL // 2
DB = D_MODEL - DA
CONV_A_WIDTH = 31
CONV_B_WIDTH = 3
N_HEADS = 8
HEAD_DIM = D_MODEL // N_HEADS // 2
ROPE_THETA = 10000.0
Q_BLOCK = 128
NORM_EPS = 1e-6
LN_EPS = 1e-5
SUBLN_EPS = 1e-5

kernel_name = 'hybrid_conv_diffattn_encoder'


def rmsnorm(x, g, eps=NORM_EPS):
    xf = x.astype(jnp.float32)
    y = xf * lax.rsqrt(jnp.mean(xf * xf, axis=-1, keepdims=True) + eps)
    return (y * g.astype(jnp.float32)).astype(x.dtype)


def layernorm(x, g, b, eps=LN_EPS):
    xf = x.astype(jnp.float32)
    mu = jnp.mean(xf, axis=-1, keepdims=True)
    var = jnp.mean(jnp.square(xf - mu), axis=-1, keepdims=True)
    y = (xf - mu) * lax.rsqrt(var + eps)
    return (y * g.astype(jnp.float32) + b.astype(jnp.float32)).astype(x.dtype)


def swiglu(x, w_gate, w_up, w_down):
    return (jax.nn.silu(x @ w_gate) * (x @ w_up)) @ w_down


def dwconv_centred(x, w):
    k, c = w.shape
    pad = (k - 1) // 2
    return lax.conv_general_dilated(
        x, w[:, None, :].astype(x.dtype), window_strides=(1,), padding=[(pad, pad)],
        dimension_numbers=('NWC', 'WIO', 'NWC'), feature_group_count=c)


def conv_mixers(h, w_in, a_dw_w, a_dw_b, a_ln_g, a_ln_b, b_dw_w, w_out):
    u = h @ w_in
    a_val, a_gate, g_b, g_c, hb = jnp.split(
        u, [DA, 2 * DA, 2 * DA + DB, 2 * DA + 2 * DB], axis=-1)
    a = a_val * jax.nn.sigmoid(a_gate)
    a = dwconv_centred(a, a_dw_w) + a_dw_b
    a = jax.nn.silu(layernorm(a, a_ln_g, a_ln_b))
    bm = g_b * dwconv_centred(g_c * hb, b_dw_w)
    return jnp.concatenate([a, bm], axis=-1) @ w_out


def rope(x, cos, sin):
    half = HEAD_DIM // 2
    x1, x2 = x[..., :half], x[..., half:]
    return jnp.concatenate([x1 * cos - x2 * sin, x2 * cos + x1 * sin], axis=-1)


def lambda_init_fn(layer_idx):
    return 0.8 - 0.6 * math.exp(-0.3 * layer_idx)


def diff_attention(h, w_qkv, w_out, lq1, lk1, lq2, lk2, subln_g, lambda_init):
    bsz, s, _ = h.shape
    qkv = h @ w_qkv
    q, k, v = jnp.split(qkv, 3, axis=-1)
    q = q.reshape(bsz, s, 2 * N_HEADS, HEAD_DIM)
    k = k.reshape(bsz, s, 2 * N_HEADS, HEAD_DIM)
    v = v.reshape(bsz, s, N_HEADS, 2 * HEAD_DIM)
    pos = jnp.arange(s, dtype=jnp.float32)
    inv_freq = 1.0 / (ROPE_THETA ** (jnp.arange(0, HEAD_DIM, 2, dtype=jnp.float32) / HEAD_DIM))
    ang = pos[:, None] * inv_freq[None, :]
    cos = jnp.cos(ang)[None, :, None, :].astype(h.dtype)
    sin = jnp.sin(ang)[None, :, None, :].astype(h.dtype)
    q = rope(q, cos, sin) * (HEAD_DIM ** -0.5)
    k = rope(k, cos, sin)
    lam = (jnp.exp(jnp.sum(lq1.astype(jnp.float32) * lk1.astype(jnp.float32)))
           - jnp.exp(jnp.sum(lq2.astype(jnp.float32) * lk2.astype(jnp.float32)))
           + lambda_init)
    nb = s // Q_BLOCK
    qb = q.reshape(bsz, nb, Q_BLOCK, 2 * N_HEADS, HEAD_DIM).transpose(1, 0, 2, 3, 4)

    def block(q_blk):
        sc = jnp.einsum('bqhd,bkhd->bhqk', q_blk, k).astype(jnp.float32)
        p = jax.nn.softmax(sc, axis=-1).reshape(bsz, N_HEADS, 2, Q_BLOCK, s)
        a = p[:, :, 0] - lam * p[:, :, 1]
        return jnp.einsum('bhqk,bkhe->bqhe', a.astype(v.dtype), v)

    o = lax.map(block, qb)
    o = o.transpose(1, 0, 2, 3, 4).reshape(bsz, s, N_HEADS, 2 * HEAD_DIM)
    o = rmsnorm(o, subln_g, SUBLN_EPS) * (1.0 - lambda_init)
    return o.reshape(bsz, s, N_HEADS * 2 * HEAD_DIM) @ w_out


def trunk(x, ffn1_gate, ffn1_up, ffn1_down, ffn2_gate, ffn2_up, ffn2_down,
          norm_ffn1, norm_mix, norm_ffn2,
          cv_w_in, a_dw_w, a_dw_b, a_ln_g, a_ln_b, b_dw_w, cv_w_out,
          at_w_qkv, at_w_out, lam_q1, lam_k1, lam_q2, lam_k2, subln_g, final_norm):
    for i in range(DEPTH):
        x = x + 0.5 * swiglu(rmsnorm(x, norm_ffn1[i]), ffn1_gate[i], ffn1_up[i], ffn1_down[i])
        h = rmsnorm(x, norm_mix[i])
        j = i // 2
        if i % 2 == 0:
            x = x + conv_mixers(h, cv_w_in[j], a_dw_w[j], a_dw_b[j], a_ln_g[j], a_ln_b[j],
                                b_dw_w[j], cv_w_out[j])
        else:
            x = x + diff_attention(h, at_w_qkv[j], at_w_out[j], lam_q1[j], lam_k1[j],
                                   lam_q2[j], lam_k2[j], subln_g[j], lambda_init_fn(i))
        x = x + 0.5 * swiglu(rmsnorm(x, norm_ffn2[i]), ffn2_gate[i], ffn2_up[i], ffn2_down[i])
    return rmsnorm(x, final_norm)


def setup_inputs(seed: int = 0) -> dict:
    key = jax.random.key(seed)
    ks = jax.random.split(key, 32)
    f32 = jnp.float32

    def nrm(k, shape, scale):
        return jax.random.normal(k, shape, f32) * scale

    def gain(k, shape):
        return 1.0 + 0.02 * jax.random.normal(k, shape, f32)

    D, F = D_MODEL, D_FF
    return {
        'x_prompt': jax.random.normal(ks[0], (BATCH, SEQ, D), f32),
        'x_sample': jax.random.normal(ks[1], (DEC_BATCH, DEC_SEQ, D), f32),
        'ffn1_gate': nrm(ks[2], (DEPTH, D, F), D ** -0.5),
        'ffn1_up': nrm(ks[3], (DEPTH, D, F), D ** -0.5),
        'ffn1_down': nrm(ks[4], (DEPTH, F, D), F ** -0.5),
        'ffn2_gate': nrm(ks[5], (DEPTH, D, F), D ** -0.5),
        'ffn2_up': nrm(ks[6], (DEPTH, D, F), D ** -0.5),
        'ffn2_down': nrm(ks[7], (DEPTH, F, D), F ** -0.5),
        'norm_ffn1': gain(ks[8], (DEPTH, D)),
        'norm_mix': gain(ks[9], (DEPTH, D)),
        'norm_ffn2': gain(ks[10], (DEPTH, D)),
        'cv_w_in': nrm(ks[11], (N_EVEN, D, 2 * DA + 3 * DB), D ** -0.5),
        'a_dw_w': nrm(ks[12], (N_EVEN, CONV_A_WIDTH, DA), CONV_A_WIDTH ** -0.5),
        'a_dw_b': nrm(ks[13], (N_EVEN, DA), 0.02),
        'a_ln_g': gain(ks[14], (N_EVEN, DA)),
        'a_ln_b': nrm(ks[15], (N_EVEN, DA), 0.02),
        'b_dw_w': nrm(ks[16], (N_EVEN, CONV_B_WIDTH, DB), CONV_B_WIDTH ** -0.5),
        'cv_w_out': nrm(ks[17], (N_EVEN, DA + DB, D), (DA + DB) ** -0.5),
        'at_w_qkv': nrm(ks[18], (N_ODD, D, 3 * D), D ** -0.5),
        'at_w_out': nrm(ks[19], (N_ODD, D, D), D ** -0.5),
        'lam_q1': nrm(ks[20], (N_ODD, HEAD_DIM), 0.1),
        'lam_k1': nrm(ks[21], (N_ODD, HEAD_DIM), 0.1),
        'lam_q2': nrm(ks[22], (N_ODD, HEAD_DIM), 0.1),
        'lam_k2': nrm(ks[23], (N_ODD, HEAD_DIM), 0.1),
        'subln_g': gain(ks[24], (N_ODD, 2 * HEAD_DIM)),
        'final_norm': gain(ks[25], (D,)),
    }


def reference(x_prompt, x_sample, ffn1_gate, ffn1_up, ffn1_down, ffn2_gate, ffn2_up, ffn2_down,
              norm_ffn1, norm_mix, norm_ffn2, cv_w_in, a_dw_w, a_dw_b, a_ln_g, a_ln_b, b_dw_w,
              cv_w_out, at_w_qkv, at_w_out, lam_q1, lam_k1, lam_q2, lam_k2, subln_g, final_norm):
    y_prompt = trunk(x_prompt, ffn1_gate, ffn1_up, ffn1_down, ffn2_gate, ffn2_up, ffn2_down,
                     norm_ffn1, norm_mix, norm_ffn2, cv_w_in, a_dw_w, a_dw_b, a_ln_g, a_ln_b,
                     b_dw_w, cv_w_out, at_w_qkv, at_w_out, lam_q1, lam_k1, lam_q2, lam_k2,
                     subln_g, final_norm)
    y_sample = trunk(x_sample, ffn1_gate, ffn1_up, ffn1_down, ffn2_gate, ffn2_up, ffn2_down,
                     norm_ffn1, norm_mix, norm_ffn2, cv_w_in, a_dw_w, a_dw_b, a_ln_g, a_ln_b,
                     b_dw_w, cv_w_out, at_w_qkv, at_w_out, lam_q1, lam_k1, lam_q2, lam_k2,
                     subln_g, final_norm)
    return (y_prompt, y_sample)
```

```python
import functools
import math

import jax
import jax.numpy as jnp
from jax import lax
from jax.experimental import pallas as pl
from jax.experimental.pallas import tpu as pltpu

D_MODEL = 1024
D_FF = 2816
DA = D_MODEL // 2
DB = D_MODEL - DA
CONV_A_WIDTH = 31
CONV_B_WIDTH = 3
N_HEADS = 8
HEAD_DIM = 64
V_DIM = 2 * HEAD_DIM
ROPE_THETA = 10000.0
NORM_EPS = 1e-6
LN_EPS = 1e-5
SUBLN_EPS = 1e-5
LOG2_E = 1.4426950408889634

V7X_LANES = 128
V7X_MXU_DIM = 256
V7X_BF16_SUBLANES = 16
V7X_VMEM_BYTES = 64 * 1024 * 1024

BF16 = jnp.bfloat16
F32 = jnp.float32


class Tiles:
    ffn_rows = 512
    ffn_cols = V7X_MXU_DIM
    conv_rows = 512
    conv_halo = V7X_BF16_SUBLANES
    conv_chunk = 64
    qkv_rows = 512
    attn_q = 256
    vmem_limit = V7X_VMEM_BYTES - 8 * 1024 * 1024


def _const_spec(shape):
    zeros = (0,) * len(shape)
    return pl.BlockSpec(shape, lambda *_: zeros, pipeline_mode=pl.Buffered(1))


def _params(*semantics):
    return pltpu.CompilerParams(dimension_semantics=semantics,
                                vmem_limit_bytes=Tiles.vmem_limit)


def _rmsnorm(x, g, eps):
    return x * lax.rsqrt(jnp.mean(x * x, axis=-1, keepdims=True) + eps) * g


def _dot(a, b):
    return jnp.dot(a, b, preferred_element_type=F32)


def _swiglu_half_step(x, g_ref, wg_ref, wu_ref, wd_ref, h_ref):
    xn = _rmsnorm(x, g_ref[...], NORM_EPS).astype(BF16)
    fc = Tiles.ffn_cols
    for c in range(D_FF // fc):
        cols = slice(c * fc, (c + 1) * fc)
        gate = _dot(xn, wg_ref[:, cols])
        up = _dot(xn, wu_ref[:, cols])
        h_ref[:, cols] = (gate * jax.nn.sigmoid(gate) * up).astype(BF16)
    return x + 0.5 * _dot(h_ref[...], wd_ref[...])


def _ffn_kernel(x_ref, g_ref, wg_ref, wu_ref, wd_ref, o_ref, h_ref):
    o_ref[...] = _swiglu_half_step(x_ref[...], g_ref, wg_ref, wu_ref, wd_ref, h_ref)


def _attn_out_ffn_kernel(x_ref, a_ref, wo_ref, g_ref, wg_ref, wu_ref, wd_ref, fg_ref,
                         o_ref, h_ref):
    x = x_ref[...] + _dot(a_ref[...], wo_ref[...])
    y = _swiglu_half_step(x, g_ref, wg_ref, wu_ref, wd_ref, h_ref)
    o_ref[...] = _rmsnorm(y, fg_ref[...], NORM_EPS)


def _ffn(x, g, wg, wu, wd):
    n, d = x.shape
    tm = Tiles.ffn_rows
    row = pl.BlockSpec((tm, d), lambda i: (i, 0))
    return pl.pallas_call(
        _ffn_kernel,
        grid=(n // tm,),
        in_specs=[row, _const_spec(g.shape), _const_spec(wg.shape), _const_spec(wu.shape),
                  _const_spec(wd.shape)],
        out_specs=row,
        out_shape=jax.ShapeDtypeStruct((n, d), F32),
        scratch_shapes=[pltpu.VMEM((tm, D_FF), BF16)],
        compiler_params=_params("parallel"),
        name="ffn",
    )(x, g, wg, wu, wd)


def _attn_out_ffn(x, a, wo, g, wg, wu, wd, fg):
    n, d = x.shape
    tm = Tiles.ffn_rows
    row = pl.BlockSpec((tm, d), lambda i: (i, 0))
    return pl.pallas_call(
        _attn_out_ffn_kernel,
        grid=(n // tm,),
        in_specs=[row, row, _const_spec(wo.shape), _const_spec(g.shape), _const_spec(wg.shape),
                  _const_spec(wu.shape), _const_spec(wd.shape), _const_spec(fg.shape)],
        out_specs=row,
        out_shape=jax.ShapeDtypeStruct((n, d), F32),
        scratch_shapes=[pltpu.VMEM((tm, D_FF), BF16)],
        compiler_params=_params("parallel"),
        name="attn_out_ffn",
    )(x, a, wo, g, wg, wu, wd, fg)


def _conv_kernel(xp_ref, x_ref, xn_ref, g_ref, win_ref, aw_ref, ab_ref, lg_ref, lb_ref,
                 bw_ref, wout_ref, o_ref, h_ref, a_ref, c_ref, gb_ref, cat_ref):
    t = pl.program_id(1)
    nt = pl.num_programs(1)
    tm = Tiles.conv_rows
    halo = Tiles.conv_halo
    g = g_ref[...]
    x = x_ref[...]

    prev = _rmsnorm(xp_ref[...], g, NORM_EPS) * (t > 0).astype(F32)
    nxt = _rmsnorm(xn_ref[...], g, NORM_EPS) * (t < nt - 1).astype(F32)
    h_ref[0:halo, :] = prev.astype(BF16)
    h_ref[halo:halo + tm, :] = _rmsnorm(x, g, NORM_EPS).astype(BF16)
    h_ref[halo + tm:, :] = nxt.astype(BF16)

    hall = h_ref[...]
    a_val = _dot(hall, win_ref[:, 0:DA])
    a_gate = _dot(hall, win_ref[:, DA:2 * DA])
    a_ref[...] = a_val * jax.nn.sigmoid(a_gate)
    g_c = _dot(hall, win_ref[:, 2 * DA + DB:2 * DA + 2 * DB])
    hb = _dot(hall, win_ref[:, 2 * DA + 2 * DB:])
    c_ref[...] = g_c * hb
    gb_ref[...] = _dot(h_ref[halo:halo + tm, :], win_ref[:, 2 * DA:2 * DA + DB])

    pad_a = (CONV_A_WIDTH - 1) // 2
    pad_b = (CONV_B_WIDTH - 1) // 2
    rc = Tiles.conv_chunk
    for r in range(tm // rc):
        r0 = r * rc
        blocks = []
        for cb in range(DA // V7X_LANES):
            lanes = slice(cb * V7X_LANES, (cb + 1) * V7X_LANES)
            acc = jnp.zeros((rc, V7X_LANES), F32)
            for k in range(CONV_A_WIDTH):
                lo = r0 + halo - pad_a + k
                acc = acc + aw_ref[k:k + 1, lanes] * a_ref[lo:lo + rc, lanes]
            blocks.append(acc + ab_ref[:, lanes])
        conv = jnp.concatenate(blocks, axis=-1)
        mu = jnp.mean(conv, axis=-1, keepdims=True)
        cen = conv - mu
        var = jnp.mean(cen * cen, axis=-1, keepdims=True)
        ln = cen * lax.rsqrt(var + LN_EPS) * lg_ref[...] + lb_ref[...]
        cat_ref[r0:r0 + rc, 0:DA] = (ln * jax.nn.sigmoid(ln)).astype(BF16)
        acc = jnp.zeros((rc, DB), F32)
        for k in range(CONV_B_WIDTH):
            lo = r0 + halo - pad_b + k
            acc = acc + bw_ref[k:k + 1, :] * c_ref[lo:lo + rc, :]
        cat_ref[r0:r0 + rc, DA:] = (gb_ref[r0:r0 + rc, :] * acc).astype(BF16)

    o_ref[...] = x + _dot(cat_ref[...], wout_ref[...])


def _conv_mixers(x, g, win, aw, ab, lg, lb, bw, wout):
    b, s, d = x.shape
    tm = Tiles.conv_rows
    halo = Tiles.conv_halo
    per = tm // halo
    last = s // halo - 1
    main = pl.BlockSpec((None, tm, d), lambda i, t: (i, t, 0))
    prev = pl.BlockSpec((None, halo, d), lambda i, t: (i, jnp.maximum(t * per - 1, 0), 0))
    nxt = pl.BlockSpec((None, halo, d), lambda i, t: (i, jnp.minimum((t + 1) * per, last), 0))
    ext = tm + 2 * halo
    return pl.pallas_call(
        _conv_kernel,
        grid=(b, s // tm),
        in_specs=[prev, main, nxt] + [_const_spec(w.shape)
                                      for w in (g, win, aw, ab, lg, lb, bw, wout)],
        out_specs=main,
        out_shape=jax.ShapeDtypeStruct(x.shape, F32),
        scratch_shapes=[pltpu.VMEM((ext, d), BF16),
                        pltpu.VMEM((ext, DA), F32),
                        pltpu.VMEM((ext, DB), F32),
                        pltpu.VMEM((tm, DB), F32),
                        pltpu.VMEM((tm, DA + DB), BF16)],
        compiler_params=_params("parallel", "parallel"),
        name="conv_mixers",
    )(x, x, x, g, win, aw, ab, lg, lb, bw, wout)


def _rope(x, cos, sin_signed):
    half = HEAD_DIM // 2
    lane = lax.broadcasted_iota(jnp.int32, x.shape, 1)
    partner = jnp.where(lane % HEAD_DIM < half,
                        pltpu.roll(x, V7X_LANES - half, 1),
                        pltpu.roll(x, half, 1))
    return x * cos + partner * sin_signed


def _qkv_kernel(x_ref, g_ref, wq_ref, wk_ref, wvt_ref, cos_ref, sin_ref, q_ref, k_ref, vt_ref):
    xn = _rmsnorm(x_ref[...], g_ref[...], NORM_EPS).astype(BF16)
    cos = cos_ref[...]
    sin = sin_ref[...]
    q_scale = HEAD_DIM ** -0.5 * LOG2_E
    for cb in range(D_MODEL // V7X_LANES):
        lanes = slice(cb * V7X_LANES, (cb + 1) * V7X_LANES)
        q = _rope(_dot(xn, wq_ref[:, lanes]), cos, sin)
        q_ref[:, lanes] = (q * q_scale).astype(BF16)
        k = _rope(_dot(xn, wk_ref[:, lanes]), cos, sin)
        k_ref[:, lanes] = k.astype(BF16)
    vt = lax.dot_general(wvt_ref[...], xn, (((1,), (1,)), ((), ())), preferred_element_type=F32)
    vt_ref[...] = vt.astype(BF16)


def _qkv(x, g, wq, wk, wvt, cos, sin):
    b, s, d = x.shape
    tm = Tiles.qkv_rows
    rows = pl.BlockSpec((None, tm, d), lambda i, t: (i, t, 0))
    table = pl.BlockSpec((tm, V7X_LANES), lambda i, t: (t, 0))
    return pl.pallas_call(
        _qkv_kernel,
        grid=(b, s // tm),
        in_specs=[rows, _const_spec(g.shape), _const_spec(wq.shape), _const_spec(wk.shape),
                  _const_spec(wvt.shape), table, table],
        out_specs=[rows, rows, pl.BlockSpec((None, d, tm), lambda i, t: (i, 0, t))],
        out_shape=[jax.ShapeDtypeStruct((b, s, d), BF16),
                   jax.ShapeDtypeStruct((b, s, d), BF16),
                   jax.ShapeDtypeStruct((b, d, s), BF16)],
        compiler_params=_params("parallel", "parallel"),
        name="qkv_rope",
    )(x, g, wq, wk, wvt, cos, sin)


def _attn_kernel(q_ref, k_ref, vt_ref, lq1_ref, lk1_ref, lq2_ref, lk2_ref, sg_ref, o_ref, *,
                 lambda_init):
    s = k_ref.shape[0]
    tq = Tiles.attn_q
    lam = (jnp.exp(jnp.sum(lq1_ref[...] * lk1_ref[...], axis=-1, keepdims=True))
           - jnp.exp(jnp.sum(lq2_ref[...] * lk2_ref[...], axis=-1, keepdims=True))
           + lambda_init)
    gain = sg_ref[...] * (1.0 - lambda_init)
    lane = lax.broadcasted_iota(jnp.int32, (tq, V_DIM), 1)

    def softmax_v(qm):
        st = lax.dot_general(k_ref[...], qm, (((1,), (1,)), ((), ())),
                             preferred_element_type=F32)
        m = jnp.max(st, axis=0, keepdims=True)
        e = jnp.exp2(st - m)
        l = jnp.sum(e, axis=0, keepdims=True)
        ot = _dot(vt_ref[...], e.astype(BF16))
        return ot / l

    def body(i, carry):
        r0 = pl.multiple_of(i * tq, tq)
        q = q_ref[pl.ds(r0, tq), :]
        zero = jnp.zeros_like(q)
        o1 = softmax_v(jnp.where(lane < HEAD_DIM, q, zero))
        o2 = softmax_v(jnp.where(lane >= HEAD_DIM, q, zero))
        ot = o1 - lam * o2
        ms = jnp.mean(ot * ot, axis=0, keepdims=True)
        y = (ot * lax.rsqrt(ms + SUBLN_EPS)).T * gain
        o_ref[pl.ds(r0, tq), :] = y.astype(o_ref.dtype)
        return carry

    lax.fori_loop(0, s // tq, body, 0)


def _diff_attention(q, k, vt, lq1, lk1, lq2, lk2, sg, lambda_init):
    b, s, d = q.shape
    rows = pl.BlockSpec((None, s, V_DIM), lambda i, h: (i, 0, h))
    return pl.pallas_call(
        functools.partial(_attn_kernel, lambda_init=lambda_init),
        grid=(b, N_HEADS),
        in_specs=[rows, rows, pl.BlockSpec((None, V_DIM, s), lambda i, h: (i, h, 0))]
                 + [_const_spec(w.shape) for w in (lq1, lk1, lq2, lk2, sg)],
        out_specs=rows,
        out_shape=jax.ShapeDtypeStruct((b, s, d), BF16),
        compiler_params=_params("parallel", "parallel"),
        name="diff_attention",
    )(q, k, vt, lq1, lk1, lq2, lk2, sg)


def _rope_tables(s):
    pos = jnp.arange(s, dtype=F32)
    inv_freq = 1.0 / (ROPE_THETA ** (jnp.arange(0, HEAD_DIM, 2, dtype=F32) / HEAD_DIM))
    ang = pos[:, None] * inv_freq[None, :]
    cos = jnp.cos(ang).astype(F32)
    sin = jnp.sin(ang).astype(F32)
    reps = V7X_LANES // HEAD_DIM
    return (jnp.tile(jnp.concatenate([cos, cos], axis=-1), (1, reps)),
            jnp.tile(jnp.concatenate([-sin, sin], axis=-1), (1, reps)))


def _lambda_init(layer_idx):
    return 0.8 - 0.6 * math.exp(-0.3 * layer_idx)


def _trunk(x, w):
    b, s, d = x.shape
    row = lambda v: v.reshape(1, -1)
    flat = lambda v: v.reshape(b * s, d)

    x = _ffn(flat(x), row(w["norm_ffn1"][0]), w["ffn1_gate"][0], w["ffn1_up"][0],
             w["ffn1_down"][0])
    x = _conv_mixers(x.reshape(b, s, d), row(w["norm_mix"][0]), w["cv_w_in"][0], w["a_dw_w"][0],
                     row(w["a_dw_b"][0]), row(w["a_ln_g"][0]), row(w["a_ln_b"][0]),
                     w["b_dw_w"][0], w["cv_w_out"][0])
    x = _ffn(flat(x), row(w["norm_ffn2"][0]), w["ffn2_gate"][0], w["ffn2_up"][0],
             w["ffn2_down"][0])
    x = _ffn(x, row(w["norm_ffn1"][1]), w["ffn1_gate"][1], w["ffn1_up"][1], w["ffn1_down"][1])

    cos, sin = _rope_tables(s)
    q, k, vt = _qkv(x.reshape(b, s, d), row(w["norm_mix"][1]), w["wq"], w["wk"], w["wvt"],
                    cos, sin)
    a = _diff_attention(q, k, vt, row(w["lam_q1"][0]), row(w["lam_k1"][0]), row(w["lam_q2"][0]),
                        row(w["lam_k2"][0]), row(w["subln_g"][0]), _lambda_init(1))
    y = _attn_out_ffn(x, flat(a), w["at_w_out"][0], row(w["norm_ffn2"][1]), w["ffn2_gate"][1],
                      w["ffn2_up"][1], w["ffn2_down"][1], row(w["final_norm"]))
    return y.reshape(b, s, d)


def kernel(x_prompt, x_sample, ffn1_gate, ffn1_up, ffn1_down, ffn2_gate, ffn2_up, ffn2_down,
           norm_ffn1, norm_mix, norm_ffn2, cv_w_in, a_dw_w, a_dw_b, a_ln_g, a_ln_b, b_dw_w,
           cv_w_out, at_w_qkv, at_w_out, lam_q1, lam_k1, lam_q2, lam_k2, subln_g, final_norm):
    wqkv = at_w_qkv[0]
    w = dict(
        ffn1_gate=ffn1_gate.astype(BF16), ffn1_up=ffn1_up.astype(BF16),
        ffn1_down=ffn1_down.astype(BF16), ffn2_gate=ffn2_gate.astype(BF16),
        ffn2_up=ffn2_up.astype(BF16), ffn2_down=ffn2_down.astype(BF16),
        norm_ffn1=norm_ffn1, norm_mix=norm_mix, norm_ffn2=norm_ffn2,
        cv_w_in=cv_w_in.astype(BF16), a_dw_w=a_dw_w, a_dw_b=a_dw_b, a_ln_g=a_ln_g,
        a_ln_b=a_ln_b, b_dw_w=b_dw_w, cv_w_out=cv_w_out.astype(BF16),
        wq=wqkv[:, :D_MODEL].astype(BF16), wk=wqkv[:, D_MODEL:2 * D_MODEL].astype(BF16),
        wvt=wqkv[:, 2 * D_MODEL:].T.astype(BF16),
        at_w_out=at_w_out.astype(BF16), lam_q1=lam_q1, lam_k1=lam_k1, lam_q2=lam_q2,
        lam_k2=lam_k2, subln_g=subln_g, final_norm=final_norm,
    )
    return _trunk(x_prompt, w), _trunk(x_sample, w)
```

```python
import functools
import math

import jax
import jax.numpy as jnp
from jax import lax
from jax.experimental import pallas as pl
from jax.experimental.pallas import tpu as pltpu

D_MODEL = 1024
D_FF = 2816
DA = D_MODEL // 2
DB = D_MODEL - DA
CONV_A_WIDTH = 31
CONV_B_WIDTH = 3
N_HEADS = 8
HEAD_DIM = 64
V_DIM = 2 * HEAD_DIM
ROPE_THETA = 10000.0
NORM_EPS = 1e-6
LN_EPS = 1e-5
SUBLN_EPS = 1e-5
LOG2_E = 1.4426950408889634

V7X_LANES = 128
V7X_MXU_DIM = 256
V7X_SUBLANES = 8
V7X_BF16_SUBLANES = 16
V7X_VMEM_BYTES = 64 * 1024 * 1024

BF16 = jnp.bfloat16
F32 = jnp.float32


class Tiles:
    ffn_rows = 512
    ffn_cols = V7X_MXU_DIM
    conv_rows = 512
    conv_halo = V7X_BF16_SUBLANES
    conv_chunk = 64
    qkv_rows = 512
    attn_q = 256
    vmem_limit = V7X_VMEM_BYTES - 8 * 1024 * 1024


def _const_spec(shape):
    zeros = (0,) * len(shape)
    return pl.BlockSpec(shape, lambda *_: zeros, pipeline_mode=pl.Buffered(1))


def _params(*semantics):
    return pltpu.CompilerParams(dimension_semantics=semantics,
                                vmem_limit_bytes=Tiles.vmem_limit)


def _rmsnorm(x, g, eps):
    return x * lax.rsqrt(jnp.mean(x * x, axis=-1, keepdims=True) + eps) * g


def _dot(a, b):
    return jnp.dot(a, b, preferred_element_type=F32)


def _swiglu_half_step(x, g_ref, wg_ref, wu_ref, wd_ref, h_ref):
    xn = _rmsnorm(x, g_ref[...], NORM_EPS).astype(BF16)
    fc = Tiles.ffn_cols
    for c in range(D_FF // fc):
        cols = slice(c * fc, (c + 1) * fc)
        gate = _dot(xn, wg_ref[:, cols])
        up = _dot(xn, wu_ref[:, cols])
        h_ref[:, cols] = (gate * jax.nn.sigmoid(gate) * up).astype(BF16)
    return x + 0.5 * _dot(h_ref[...], wd_ref[...])


def _ffn_kernel(x_ref, g_ref, wg_ref, wu_ref, wd_ref, o_ref, h_ref):
    o_ref[...] = _swiglu_half_step(x_ref[...], g_ref, wg_ref, wu_ref, wd_ref, h_ref)


def _attn_out_ffn_kernel(x_ref, a_ref, wo_ref, g_ref, wg_ref, wu_ref, wd_ref, fg_ref,
                         o_ref, h_ref):
    x = x_ref[...] + _dot(a_ref[...], wo_ref[...])
    y = _swiglu_half_step(x, g_ref, wg_ref, wu_ref, wd_ref, h_ref)
    o_ref[...] = _rmsnorm(y, fg_ref[...], NORM_EPS)


def _ffn(x, g, wg, wu, wd):
    n, d = x.shape
    tm = Tiles.ffn_rows
    row = pl.BlockSpec((tm, d), lambda i: (i, 0))
    return pl.pallas_call(
        _ffn_kernel,
        grid=(n // tm,),
        in_specs=[row, _const_spec(g.shape), _const_spec(wg.shape), _const_spec(wu.shape),
                  _const_spec(wd.shape)],
        out_specs=row,
        out_shape=jax.ShapeDtypeStruct((n, d), F32),
        scratch_shapes=[pltpu.VMEM((tm, D_FF), BF16)],
        compiler_params=_params("parallel"),
        name="ffn",
    )(x, g, wg, wu, wd)


def _attn_out_ffn(x, a, wo, g, wg, wu, wd, fg):
    n, d = x.shape
    tm = Tiles.ffn_rows
    row = pl.BlockSpec((tm, d), lambda i: (i, 0))
    return pl.pallas_call(
        _attn_out_ffn_kernel,
        grid=(n // tm,),
        in_specs=[row, row, _const_spec(wo.shape), _const_spec(g.shape), _const_spec(wg.shape),
                  _const_spec(wu.shape), _const_spec(wd.shape), _const_spec(fg.shape)],
        out_specs=row,
        out_shape=jax.ShapeDtypeStruct((n, d), F32),
        scratch_shapes=[pltpu.VMEM((tm, D_FF), BF16)],
        compiler_params=_params("parallel"),
        name="attn_out_ffn",
    )(x, a, wo, g, wg, wu, wd, fg)


def _shifted_taps(src_ref, w_ref, lanes, first_row, rows, width):
    sub = V7X_SUBLANES
    base = first_row - first_row % sub
    out = None
    for res in range(sub):
        acc = None
        for k in range(width):
            off = first_row + k - base
            if off % sub != res:
                continue
            lo = base + off - res
            term = w_ref[k:k + 1, lanes] * src_ref[lo:lo + rows + sub, lanes]
            acc = term if acc is None else acc + term
        if acc is None:
            continue
        part = acc[res:res + rows]
        out = part if out is None else out + part
    return out


def _conv_kernel(xp_ref, x_ref, xn_ref, g_ref, win_ref, aw_ref, ab_ref, lg_ref, lb_ref,
                 bw_ref, wout_ref, o_ref, h_ref, a_ref, c_ref, gb_ref, cat_ref):
    t = pl.program_id(1)
    nt = pl.num_programs(1)
    tm = Tiles.conv_rows
    halo = Tiles.conv_halo
    g = g_ref[...]
    x = x_ref[...]

    prev = _rmsnorm(xp_ref[...], g, NORM_EPS) * (t > 0).astype(F32)
    nxt = _rmsnorm(xn_ref[...], g, NORM_EPS) * (t < nt - 1).astype(F32)
    h_ref[0:halo, :] = prev.astype(BF16)
    h_ref[halo:halo + tm, :] = _rmsnorm(x, g, NORM_EPS).astype(BF16)
    h_ref[halo + tm:, :] = nxt.astype(BF16)

    hall = h_ref[...]
    a_val = _dot(hall, win_ref[:, 0:DA])
    a_gate = _dot(hall, win_ref[:, DA:2 * DA])
    a_ref[...] = a_val * jax.nn.sigmoid(a_gate)
    g_c = _dot(hall, win_ref[:, 2 * DA + DB:2 * DA + 2 * DB])
    hb = _dot(hall, win_ref[:, 2 * DA + 2 * DB:])
    c_ref[...] = g_c * hb
    gb_ref[...] = _dot(h_ref[halo:halo + tm, :], win_ref[:, 2 * DA:2 * DA + DB])

    pad_a = (CONV_A_WIDTH - 1) // 2
    pad_b = (CONV_B_WIDTH - 1) // 2
    rc = Tiles.conv_chunk
    for r in range(tm // rc):
        r0 = r * rc
        blocks = []
        for cb in range(DA // V7X_LANES):
            lanes = slice(cb * V7X_LANES, (cb + 1) * V7X_LANES)
            blocks.append(_shifted_taps(a_ref, aw_ref, lanes, r0 + halo - pad_a, rc, CONV_A_WIDTH)
                          + ab_ref[:, lanes])
        conv = jnp.concatenate(blocks, axis=-1)
        mu = jnp.mean(conv, axis=-1, keepdims=True)
        cen = conv - mu
        var = jnp.mean(cen * cen, axis=-1, keepdims=True)
        ln = cen * lax.rsqrt(var + LN_EPS) * lg_ref[...] + lb_ref[...]
        cat_ref[r0:r0 + rc, 0:DA] = (ln * jax.nn.sigmoid(ln)).astype(BF16)
        for cb in range(DB // V7X_LANES):
            lanes = slice(cb * V7X_LANES, (cb + 1) * V7X_LANES)
            acc = _shifted_taps(c_ref, bw_ref, lanes, r0 + halo - pad_b, rc, CONV_B_WIDTH)
            cat_ref[r0:r0 + rc, DA + cb * V7X_LANES:DA + (cb + 1) * V7X_LANES] = (
                gb_ref[r0:r0 + rc, lanes] * acc).astype(BF16)

    o_ref[...] = x + _dot(cat_ref[...], wout_ref[...])


def _conv_mixers(x, g, win, aw, ab, lg, lb, bw, wout):
    b, s, d = x.shape
    tm = Tiles.conv_rows
    halo = Tiles.conv_halo
    per = tm // halo
    last = s // halo - 1
    main = pl.BlockSpec((None, tm, d), lambda i, t: (i, t, 0))
    prev = pl.BlockSpec((None, halo, d), lambda i, t: (i, jnp.maximum(t * per - 1, 0), 0))
    nxt = pl.BlockSpec((None, halo, d), lambda i, t: (i, jnp.minimum((t + 1) * per, last), 0))
    ext = tm + 2 * halo
    return pl.pallas_call(
        _conv_kernel,
        grid=(b, s // tm),
        in_specs=[prev, main, nxt] + [_const_spec(w.shape)
                                      for w in (g, win, aw, ab, lg, lb, bw, wout)],
        out_specs=main,
        out_shape=jax.ShapeDtypeStruct(x.shape, F32),
        scratch_shapes=[pltpu.VMEM((ext, d), BF16),
                        pltpu.VMEM((ext, DA), F32),
                        pltpu.VMEM((ext, DB), F32),
                        pltpu.VMEM((tm, DB), F32),
                        pltpu.VMEM((tm, DA + DB), BF16)],
        compiler_params=_params("parallel", "parallel"),
        name="conv_mixers",
    )(x, x, x, g, win, aw, ab, lg, lb, bw, wout)


def _rope(x, cos, sin_signed):
    half = HEAD_DIM // 2
    lane = lax.broadcasted_iota(jnp.int32, x.shape, 1)
    partner = jnp.where(lane % HEAD_DIM < half,
                        pltpu.roll(x, V7X_LANES - half, 1),
                        pltpu.roll(x, half, 1))
    return x * cos + partner * sin_signed


def _qkv_kernel(x_ref, g_ref, wq_ref, wk_ref, wvt_ref, cos_ref, sin_ref, q_ref, k_ref, vt_ref):
    xn = _rmsnorm(x_ref[...], g_ref[...], NORM_EPS).astype(BF16)
    cos = cos_ref[...]
    sin = sin_ref[...]
    q_scale = HEAD_DIM ** -0.5 * LOG2_E
    nb = V7X_MXU_DIM
    for cb in range(D_MODEL // nb):
        cols = slice(cb * nb, (cb + 1) * nb)
        q = _dot(xn, wq_ref[:, cols])
        k = _dot(xn, wk_ref[:, cols])
        for hb in range(nb // V7X_LANES):
            part = slice(hb * V7X_LANES, (hb + 1) * V7X_LANES)
            lanes = slice(cb * nb + hb * V7X_LANES, cb * nb + (hb + 1) * V7X_LANES)
            q_ref[:, lanes] = (_rope(q[:, part], cos, sin) * q_scale).astype(BF16)
            k_ref[:, lanes] = _rope(k[:, part], cos, sin).astype(BF16)
    vt = lax.dot_general(wvt_ref[...], xn, (((1,), (1,)), ((), ())), preferred_element_type=F32)
    vt_ref[...] = vt.astype(BF16)


def _qkv(x, g, wq, wk, wvt, cos, sin):
    b, s, d = x.shape
    tm = Tiles.qkv_rows
    rows = pl.BlockSpec((None, tm, d), lambda i, t: (i, t, 0))
    table = pl.BlockSpec((tm, V7X_LANES), lambda i, t: (t, 0))
    return pl.pallas_call(
        _qkv_kernel,
        grid=(b, s // tm),
        in_specs=[rows, _const_spec(g.shape), _const_spec(wq.shape), _const_spec(wk.shape),
                  _const_spec(wvt.shape), table, table],
        out_specs=[rows, rows, pl.BlockSpec((None, d, tm), lambda i, t: (i, 0, t))],
        out_shape=[jax.ShapeDtypeStruct((b, s, d), BF16),
                   jax.ShapeDtypeStruct((b, s, d), BF16),
                   jax.ShapeDtypeStruct((b, d, s), BF16)],
        compiler_params=_params("parallel", "parallel"),
        name="qkv_rope",
    )(x, g, wq, wk, wvt, cos, sin)


def _attn_kernel(q_ref, k_ref, vt_ref, lq1_ref, lk1_ref, lq2_ref, lk2_ref, sg_ref, o_ref,
                 s0_ref, s1_ref, *, lambda_init):
    s = k_ref.shape[0]
    tq = Tiles.attn_q
    n_units = s // tq
    lam = (jnp.exp(jnp.sum(lq1_ref[...] * lk1_ref[...], axis=-1, keepdims=True))
           - jnp.exp(jnp.sum(lq2_ref[...] * lk2_ref[...], axis=-1, keepdims=True))
           + lambda_init)
    gain = sg_ref[...] * (1.0 - lambda_init)
    lane = lax.broadcasted_iota(jnp.int32, (tq, V_DIM), 1)

    def scores(i, s_ref):
        r0 = pl.multiple_of(i * tq, tq)
        q = q_ref[pl.ds(r0, tq), :]
        zero = jnp.zeros_like(q)
        qq = jnp.concatenate([jnp.where(lane < HEAD_DIM, q, zero),
                              jnp.where(lane >= HEAD_DIM, q, zero)], axis=0)
        st = lax.dot_general(k_ref[...], qq, (((1,), (1,)), ((), ())),
                             preferred_element_type=F32)
        s_ref[...] = st
        return jnp.max(st, axis=0, keepdims=True)

    def finish(i, s_ref, m):
        r0 = pl.multiple_of(i * tq, tq)
        e = jnp.exp2(s_ref[...] - m)
        l = jnp.sum(e, axis=0, keepdims=True)
        ot = _dot(vt_ref[...], e.astype(BF16)) / l
        ot = ot[:, :tq] - lam * ot[:, tq:]
        ms = jnp.mean(ot * ot, axis=0, keepdims=True)
        y = (ot * lax.rsqrt(ms + SUBLN_EPS)).T * gain
        o_ref[pl.ds(r0, tq), :] = y.astype(o_ref.dtype)

    def pair(j, m):
        i = 2 * j
        m_odd = scores(i + 1, s1_ref)
        finish(i, s0_ref, m)
        m_even = scores(i + 2, s0_ref)
        finish(i + 1, s1_ref, m_odd)
        return m_even

    m = lax.fori_loop(0, n_units // 2 - 1, pair, scores(0, s0_ref))
    m_last = scores(n_units - 1, s1_ref)
    finish(n_units - 2, s0_ref, m)
    finish(n_units - 1, s1_ref, m_last)


def _diff_attention(q, k, vt, lq1, lk1, lq2, lk2, sg, lambda_init):
    b, s, d = q.shape
    rows = pl.BlockSpec((None, s, V_DIM), lambda i, h: (i, 0, h))
    score_buf = pltpu.VMEM((s, 2 * Tiles.attn_q), F32)
    return pl.pallas_call(
        functools.partial(_attn_kernel, lambda_init=lambda_init),
        grid=(b, N_HEADS),
        in_specs=[rows, rows, pl.BlockSpec((None, V_DIM, s), lambda i, h: (i, h, 0))]
                 + [_const_spec(w.shape) for w in (lq1, lk1, lq2, lk2, sg)],
        out_specs=rows,
        out_shape=jax.ShapeDtypeStruct((b, s, d), BF16),
        scratch_shapes=[score_buf, score_buf],
        compiler_params=_params("parallel", "parallel"),
        name="diff_attention",
    )(q, k, vt, lq1, lk1, lq2, lk2, sg)


def _rope_tables(s):
    pos = jnp.arange(s, dtype=F32)
    inv_freq = 1.0 / (ROPE_THETA ** (jnp.arange(0, HEAD_DIM, 2, dtype=F32) / HEAD_DIM))
    ang = pos[:, None] * inv_freq[None, :]
    cos = jnp.cos(ang).astype(F32)
    sin = jnp.sin(ang).astype(F32)
    reps = V7X_LANES // HEAD_DIM
    return (jnp.tile(jnp.concatenate([cos, cos], axis=-1), (1, reps)),
            jnp.tile(jnp.concatenate([-sin, sin], axis=-1), (1, reps)))


def _lambda_init(layer_idx):
    return 0.8 - 0.6 * math.exp(-0.3 * layer_idx)


def _trunk(x, w):
    b, s, d = x.shape
    row = lambda v: v.reshape(1, -1)
    flat = lambda v: v.reshape(b * s, d)

    x = _ffn(flat(x), row(w["norm_ffn1"][0]), w["ffn1_gate"][0], w["ffn1_up"][0],
             w["ffn1_down"][0])
    x = _conv_mixers(x.reshape(b, s, d), row(w["norm_mix"][0]), w["cv_w_in"][0], w["a_dw_w"][0],
                     row(w["a_dw_b"][0]), row(w["a_ln_g"][0]), row(w["a_ln_b"][0]),
                     w["b_dw_w"][0], w["cv_w_out"][0])
    x = _ffn(flat(x), row(w["norm_ffn2"][0]), w["ffn2_gate"][0], w["ffn2_up"][0],
             w["ffn2_down"][0])
    x = _ffn(x, row(w["norm_ffn1"][1]), w["ffn1_gate"][1], w["ffn1_up"][1], w["ffn1_down"][1])

    cos, sin = _rope_tables(s)
    q, k, vt = _qkv(x.reshape(b, s, d), row(w["norm_mix"][1]), w["wq"], w["wk"], w["wvt"],
                    cos, sin)
    a = _diff_attention(q, k, vt, row(w["lam_q1"][0]), row(w["lam_k1"][0]), row(w["lam_q2"][0]),
                        row(w["lam_k2"][0]), row(w["subln_g"][0]), _lambda_init(1))
    y = _attn_out_ffn(x, flat(a), w["at_w_out"][0], row(w["norm_ffn2"][1]), w["ffn2_gate"][1],
                      w["ffn2_up"][1], w["ffn2_down"][1], row(w["final_norm"]))
    return y.reshape(b, s, d)


def kernel(x_prompt, x_sample, ffn1_gate, ffn1_up, ffn1_down, ffn2_gate, ffn2_up, ffn2_down,
           norm_ffn1, norm_mix, norm_ffn2, cv_w_in, a_dw_w, a_dw_b, a_ln_g, a_ln_b, b_dw_w,
           cv_w_out, at_w_qkv, at_w_out, lam_q1, lam_k1, lam_q2, lam_k2, subln_g, final_norm):
    wqkv = at_w_qkv[0]
    w = dict(
        ffn1_gate=ffn1_gate.astype(BF16), ffn1_up=ffn1_up.astype(BF16),
        ffn1_down=ffn1_down.astype(BF16), ffn2_gate=ffn2_gate.astype(BF16),
        ffn2_up=ffn2_up.astype(BF16), ffn2_down=ffn2_down.astype(BF16),
        norm_ffn1=norm_ffn1, norm_mix=norm_mix, norm_ffn2=norm_ffn2,
        cv_w_in=cv_w_in.astype(BF16), a_dw_w=a_dw_w, a_dw_b=a_dw_b, a_ln_g=a_ln_g,
        a_ln_b=a_ln_b, b_dw_w=b_dw_w, cv_w_out=cv_w_out.astype(BF16),
        wq=wqkv[:, :D_MODEL].astype(BF16), wk=wqkv[:, D_MODEL:2 * D_MODEL].astype(BF16),
        wvt=wqkv[:, 2 * D_MODEL:].T.astype(BF16),
        at_w_out=at_w_out.astype(BF16), lam_q1=lam_q1, lam_k1=lam_k1, lam_q2=lam_q2,
        lam_k2=lam_k2, subln_g=subln_g, final_norm=final_norm,
    )
    return _trunk(x_prompt, w), _trunk(x_sample, w)
```
